```python
import jax, jax.numpy as jnp
from jax import lax
import numpy as np

D_MODEL = 1024
BATCH = 8
SEQ = 8192
DEPTH = 2

N_MIXERS = 2
N_A_LAYERS = (DEPTH + 1) // 2
N_B_LAYERS = DEPTH // 2
CHUNK = 128
GM_WIDTH = 2 * D_MODEL
GM_GROUPS = 8
GM_GROUP_DIM = GM_WIDTH // GM_GROUPS
N_HEADS = 16
HEAD_DIM = D_MODEL // N_HEADS
Q_BLOCK = 128
FFN_DIM = 2 * D_MODEL
CONV_WIDTH = 3
RMS_EPS = 1e-6
LN_EPS = 1e-5
FORGET_BIAS_INIT = 4.0

kernel_name = "hybrid_gmlp_fox_convffn"


def rmsnorm(x, g):
    xf = x.astype(jnp.float32)
    y = xf * lax.rsqrt(jnp.mean(xf * xf, axis=-1, keepdims=True) + RMS_EPS)
    return (y * g.astype(jnp.float32)).astype(x.dtype)


def layernorm(x, g, b):
    xf = x.astype(jnp.float32)
    mu = jnp.mean(xf, axis=-1, keepdims=True)
    xc = xf - mu
    y = xc * lax.rsqrt(jnp.mean(xc * xc, axis=-1, keepdims=True) + LN_EPS)
    return (y * g.astype(jnp.float32) + b.astype(jnp.float32)).astype(x.dtype)


def chunked_spatial_gating_mixer(h, w_in, ln_g, ln_b, w_s, b_s, w_out):
    bsz, seq, _ = h.shape
    z = jax.nn.gelu(h @ w_in)
    u, v = jnp.split(z, 2, axis=-1)
    v = layernorm(v, ln_g, ln_b)
    n_chunks = seq // CHUNK
    v = v.reshape(bsz, n_chunks, CHUNK, GM_GROUPS, GM_GROUP_DIM)
    causal = jnp.tril(jnp.ones((CHUNK, CHUNK), dtype=bool))
    w_causal = jnp.where(causal[None], w_s, jnp.zeros((), w_s.dtype))
    s = jnp.einsum("gts,bnsgc->bntgc", w_causal, v)
    s = s + b_s.T[None, None, :, :, None]
    s = s.reshape(bsz, seq, GM_WIDTH)
    return (u * s) @ w_out


def forgetting_attention_mixer(h, w_qkvf, b_f, w_o):
    bsz, seq, _ = h.shape
    proj = h @ w_qkvf
    q, k, v, f_logit = jnp.split(proj, [D_MODEL, 2 * D_MODEL, 3 * D_MODEL], axis=-1)
    q = q.reshape(bsz, seq, N_HEADS, HEAD_DIM)
    k = k.reshape(bsz, seq, N_HEADS, HEAD_DIM)
    v = v.reshape(bsz, seq, N_HEADS, HEAD_DIM)
    log_f = jax.nn.log_sigmoid((f_logit + b_f).astype(jnp.float32))
    cum = jnp.cumsum(log_f, axis=1).transpose(0, 2, 1)
    scale = HEAD_DIM ** -0.5
    neg = jnp.finfo(jnp.float32).min
    outs = []
    for blk in range(seq // Q_BLOCK):
        q0 = blk * Q_BLOCK
        q1 = q0 + Q_BLOCK
        qb = q[:, q0:q1]
        kb = k[:, :q1]
        vb = v[:, :q1]
        logits = jnp.einsum("bqhd,bkhd->bhqk", qb, kb,
                            preferred_element_type=jnp.float32) * scale
        logits = logits + cum[:, :, q0:q1, None] - cum[:, :, None, :q1]
        mask = jnp.arange(q0, q1)[:, None] >= jnp.arange(q1)[None, :]
        logits = jnp.where(mask, logits, neg)
        p = jax.nn.softmax(logits, axis=-1)
        outs.append(jnp.einsum("bhqk,bkhd->bqhd", p.astype(vb.dtype), vb))
    o = jnp.concatenate(outs, axis=1).reshape(bsz, seq, D_MODEL)
    return o @ w_o


def conv_gated_ffn(h, w_gate, w_up, conv_w, conv_b, w_down):
    seq = h.shape[1]
    a = h @ w_gate
    a_pad = jnp.pad(a, ((0, 0), (CONV_WIDTH - 1, 0), (0, 0)))
    a = conv_b + a_pad[:, 0:seq] * conv_w[0]
    for i in range(1, CONV_WIDTH):
        a = a + a_pad[:, i:i + seq] * conv_w[i]
    return (jax.nn.silu(a) * (h @ w_up)) @ w_down


def setup_inputs(seed: int = 0) -> dict:
    key = jax.random.key(seed)
    ks = jax.random.split(key, 20)

    def normal(k, shape, scale):
        return jax.random.normal(k, shape, jnp.float32) * scale

    D, E, G, C, H, F = D_MODEL, GM_WIDTH, GM_GROUPS, CHUNK, N_HEADS, FFN_DIM
    return {
        "x": normal(ks[0], (BATCH, SEQ, D), 1.0),
        "mix_norm_g": 1.0 + normal(ks[1], (DEPTH, D), 0.02),
        "ffn_norm_g": 1.0 + normal(ks[2], (DEPTH, D), 0.02),
        "gm_w_in": normal(ks[3], (N_A_LAYERS, D, 2 * E), D ** -0.5),
        "gm_ln_g": 1.0 + normal(ks[4], (N_A_LAYERS, E), 0.02),
        "gm_ln_b": normal(ks[5], (N_A_LAYERS, E), 0.02),
        "gm_w_s": normal(ks[6], (N_A_LAYERS, G, C, C), C ** -0.5),
        "gm_b_s": 1.0 + normal(ks[7], (N_A_LAYERS, G, C), 0.02),
        "gm_w_out": normal(ks[8], (N_A_LAYERS, E, D), E ** -0.5),
        "fox_w_qkvf": normal(ks[9], (N_B_LAYERS, D, 3 * D + H), D ** -0.5),
        "fox_b_f": FORGET_BIAS_INIT + normal(ks[10], (N_B_LAYERS, H), 0.5),
        "fox_w_o": normal(ks[11], (N_B_LAYERS, D, D), D ** -0.5),
        "ffn_w_gate": normal(ks[12], (DEPTH, D, F), D ** -0.5),
        "ffn_w_up": normal(ks[13], (DEPTH, D, F), D ** -0.5),
        "ffn_conv_w": normal(ks[14], (DEPTH, CONV_WIDTH, F), CONV_WIDTH ** -0.5),
        "ffn_conv_b": normal(ks[15], (DEPTH, F), 0.01),
        "ffn_w_down": normal(ks[16], (DEPTH, F, D), F ** -0.5),
        "final_norm_g": 1.0 + normal(ks[17], (D,), 0.02),
    }


def reference(x, mix_norm_g, ffn_norm_g, gm_w_in, gm_ln_g, gm_ln_b, gm_w_s, gm_b_s,
              gm_w_out, fox_w_qkvf, fox_b_f, fox_w_o, ffn_w_gate, ffn_w_up,
              ffn_conv_w, ffn_conv_b, ffn_w_down, final_norm_g):
    h = x
    for i in range(DEPTH):
        hn = rmsnorm(h, mix_norm_g[i])
        j = i // N_MIXERS
        if i % N_MIXERS == 0:
            mix = chunked_spatial_gating_mixer(hn, gm_w_in[j], gm_ln_g[j], gm_ln_b[j],
                                               gm_w_s[j], gm_b_s[j], gm_w_out[j])
        else:
            mix = forgetting_attention_mixer(hn, fox_w_qkvf[j], fox_b_f[j], fox_w_o[j])
        h = h + mix
        hn = rmsnorm(h, ffn_norm_g[i])
        h = h + conv_gated_ffn(hn, ffn_w_gate[i], ffn_w_up[i], ffn_conv_w[i],
                               ffn_conv_b[i], ffn_w_down[i])
    return rmsnorm(h, final_norm_g)
```

```python
import functools

import jax
import jax.numpy as jnp
import numpy as np
from jax import lax
from jax.experimental import pallas as pl
from jax.experimental.pallas import tpu as pltpu

D_MODEL = 1024
GM_WIDTH = 2048
GM_GROUPS = 8
GM_GROUP_DIM = 256
CHUNK = 128
N_HEADS = 16
HEAD_DIM = 64
FFN_DIM = 2048
CONV_WIDTH = 3
RMS_EPS = 1e-6
LN_EPS = 1e-5

LANES = 128
SUBLANES = 8
HEADS_PER_BLOCK = LANES // HEAD_DIM
GATE_LANES = LANES // N_HEADS
VMEM_LIMIT_BYTES = 56 * 1024 * 1024

GMLP_ROWS = 256
FFN_ROWS = 256
QKV_ROWS = 256
ATTN_BLOCK = 256

F32 = jnp.float32
BF16 = jnp.bfloat16
NEG_BIG = -1e30


def _rmsnorm(x, g):
    ms = jnp.mean(x * x, axis=-1, keepdims=True)
    return x * lax.rsqrt(ms + RMS_EPS) * g


def _const_spec(shape):
    zeros = (0,) * len(shape)
    return pl.BlockSpec(shape, lambda *_: zeros, pipeline_mode=pl.Buffered(1))


def _params(n_grid):
    return pltpu.CompilerParams(
        dimension_semantics=("arbitrary",) * n_grid,
        vmem_limit_bytes=VMEM_LIMIT_BYTES,
    )


def _gmlp_kernel(x_ref, g_ref, win_ref, lng_ref, lnb_ref, ws_ref, bs_ref, wout_ref,
                 o_ref, us_ref):
    rows = x_ref.shape[0]
    x = x_ref[...]
    hn = _rmsnorm(x, g_ref[...]).astype(BF16)
    z = jnp.dot(hn, win_ref[...], preferred_element_type=F32)
    z = jax.nn.gelu(z)
    u = z[:, :GM_WIDTH]
    v = z[:, GM_WIDTH:]
    mu = jnp.mean(v, axis=-1, keepdims=True)
    vc = v - mu
    var = jnp.mean(vc * vc, axis=-1, keepdims=True)
    vn = (vc * lax.rsqrt(var + LN_EPS) * lng_ref[...] + lnb_ref[...]).astype(BF16)

    t_idx = lax.broadcasted_iota(jnp.int32, (CHUNK, CHUNK), 0)
    s_idx = lax.broadcasted_iota(jnp.int32, (CHUNK, CHUNK), 1)
    causal = t_idx >= s_idx
    w_causal = [jnp.where(causal, ws_ref[g], jnp.zeros((), BF16)) for g in range(GM_GROUPS)]
    for c in range(rows // CHUNK):
        r0 = c * CHUNK
        for g in range(GM_GROUPS):
            c0 = g * GM_GROUP_DIM
            s = jnp.dot(w_causal[g], vn[r0:r0 + CHUNK, c0:c0 + GM_GROUP_DIM],
                        preferred_element_type=F32) + bs_ref[g]
            us_ref[r0:r0 + CHUNK, c0:c0 + GM_GROUP_DIM] = (
                u[r0:r0 + CHUNK, c0:c0 + GM_GROUP_DIM] * s).astype(BF16)
    y = jnp.dot(us_ref[...], wout_ref[...], preferred_element_type=F32)
    o_ref[...] = x + y


def _gmlp(x2, g, w_in, ln_g, ln_b, w_s, b_s_b, w_out):
    n = x2.shape[0]
    rows = GMLP_ROWS
    return pl.pallas_call(
        _gmlp_kernel,
        grid=(n // rows,),
        in_specs=[
            pl.BlockSpec((rows, D_MODEL), lambda i: (i, 0)),
            _const_spec((1, D_MODEL)),
            _const_spec((D_MODEL, 2 * GM_WIDTH)),
            _const_spec((1, GM_WIDTH)),
            _const_spec((1, GM_WIDTH)),
            _const_spec((GM_GROUPS, CHUNK, CHUNK)),
            _const_spec((GM_GROUPS, CHUNK, GM_GROUP_DIM)),
            _const_spec((GM_WIDTH, D_MODEL)),
        ],
        out_specs=pl.BlockSpec((rows, D_MODEL), lambda i: (i, 0)),
        out_shape=jax.ShapeDtypeStruct((n, D_MODEL), F32),
        scratch_shapes=[pltpu.VMEM((rows, GM_WIDTH), BF16)],
        compiler_params=_params(1),
        name="gmlp",
    )(x2, g, w_in, ln_g, ln_b, w_s, b_s_b, w_out)


def _ffn_kernel(*refs, has_proj, final_norm):
    refs = list(refs)
    h_ref = refs.pop(0)
    if has_proj:
        o_ref = refs.pop(0)
        wo_ref = refs.pop(0)
    g_ref, wg_ref, wu_ref, cw_ref, cb_ref, wd_ref = refs[:6]
    refs = refs[6:]
    if final_norm:
        fg_ref = refs.pop(0)
    out_ref, carry_ref = refs

    rows = h_ref.shape[1]
    h = h_ref[0]
    if has_proj:
        h = h + jnp.dot(o_ref[0], wo_ref[...], preferred_element_type=F32)
    hn = _rmsnorm(h, g_ref[...]).astype(BF16)
    a = jnp.dot(hn, wg_ref[...], preferred_element_type=F32)
    up = jnp.dot(hn, wu_ref[...], preferred_element_type=F32)

    @pl.when(pl.program_id(1) == 0)
    def _():
        carry_ref[...] = jnp.zeros_like(carry_ref)

    prev = carry_ref[...]
    p1 = prev[SUBLANES - 1:SUBLANES]
    p2 = prev[SUBLANES - 2:SUBLANES - 1]
    carry_ref[...] = a[rows - SUBLANES:rows]
    row = lax.broadcasted_iota(jnp.int32, a.shape, 0)
    a_m1 = jnp.where(row == 0, p1, pltpu.roll(a, 1, 0))
    a_m2 = jnp.where(row == 0, p2, jnp.where(row == 1, p1, pltpu.roll(a, 2, 0)))
    cw = cw_ref[...]
    ac = cb_ref[...] + a_m2 * cw[0:1] + a_m1 * cw[1:2] + a * cw[2:3]
    gated = (jax.nn.silu(ac) * up).astype(BF16)
    out = h + jnp.dot(gated, wd_ref[...], preferred_element_type=F32)
    if final_norm:
        out = _rmsnorm(out, fg_ref[...])
    out_ref[0] = out


def _ffn(h, g, w_gate, w_up, conv_w, conv_b, w_down, *, proj=None, final_g=None):
    b, s, _ = h.shape
    rows = FFN_ROWS
    tile = pl.BlockSpec((1, rows, D_MODEL), lambda i, j: (i, j, 0))
    args = [h]
    specs = [tile]
    if proj is not None:
        o, w_o = proj
        args += [o, w_o]
        specs += [tile, _const_spec((D_MODEL, D_MODEL))]
    args += [g, w_gate, w_up, conv_w, conv_b, w_down]
    specs += [
        _const_spec((1, D_MODEL)),
        _const_spec((D_MODEL, FFN_DIM)),
        _const_spec((D_MODEL, FFN_DIM)),
        _const_spec((SUBLANES, FFN_DIM)),
        _const_spec((1, FFN_DIM)),
        _const_spec((FFN_DIM, D_MODEL)),
    ]
    if final_g is not None:
        args.append(final_g)
        specs.append(_const_spec((1, D_MODEL)))
    kern = functools.partial(_ffn_kernel, has_proj=proj is not None,
                             final_norm=final_g is not None)
    return pl.pallas_call(
        kern,
        grid=(b, s // rows),
        in_specs=specs,
        out_specs=tile,
        out_shape=jax.ShapeDtypeStruct((b, s, D_MODEL), F32),
        scratch_shapes=[pltpu.VMEM((SUBLANES, FFN_DIM), F32)],
        compiler_params=_params(2),
        name="ffn_proj" if proj is not None else "ffn",
    )(*args)


def _split3(x):
    p1 = x.astype(BF16)
    r1 = x - p1.astype(F32)
    p2 = r1.astype(BF16)
    p3 = (r1 - p2.astype(F32)).astype(BF16)
    return jnp.concatenate([p1, p2, p3], axis=1)


def _qkvf_kernel(h_ref, g_ref, wq_ref, wk_ref, wvt_ref, wf_ref, bf_ref, eq_ref, ek_ref,
                 cq_ref, ck_ref, q_ref, qc_ref, k_ref, kc_ref, vt_ref, carry_ref):
    rows = h_ref.shape[1]

    @pl.when(pl.program_id(1) == 0)
    def _():
        carry_ref[...] = jnp.zeros_like(carry_ref)

    hn = _rmsnorm(h_ref[0], g_ref[...]).astype(BF16)
    q = jnp.dot(hn, wq_ref[...], preferred_element_type=F32) * (HEAD_DIM ** -0.5)
    q_ref[0] = q.astype(BF16)
    k_ref[0] = jnp.dot(hn, wk_ref[...], preferred_element_type=F32).astype(BF16)
    vt = lax.dot_general(wvt_ref[...], hn, (((1,), (1,)), ((), ())),
                         preferred_element_type=F32)
    vt_ref[0, 0] = vt.astype(BF16)

    f_logit = jnp.dot(hn, wf_ref[...], preferred_element_type=F32) + bf_ref[...]
    log_f = jnp.minimum(f_logit, 0.0) - jnp.log1p(jnp.exp(-jnp.abs(f_logit)))
    lane = lax.broadcasted_iota(jnp.int32, log_f.shape, 1)
    log_f = jnp.where(lane < N_HEADS, log_f, 0.0)
    t_idx = lax.broadcasted_iota(jnp.int32, (rows, rows), 0)
    s_idx = lax.broadcasted_iota(jnp.int32, (rows, rows), 1)
    tri = jnp.where(t_idx >= s_idx, 1.0, 0.0).astype(BF16)
    part = jnp.dot(tri, _split3(log_f), preferred_element_type=F32)
    cum = carry_ref[0:1] + (part[:, :LANES] + part[:, LANES:2 * LANES] + part[:, 2 * LANES:])
    carry_ref[...] = jnp.broadcast_to(cum[rows - 1:rows], carry_ref.shape)
    pieces = _split3(cum)
    qc_ref[0] = (jnp.dot(pieces, eq_ref[...], preferred_element_type=F32) + cq_ref[...]).astype(BF16)
    kc_ref[0] = (jnp.dot(pieces, ek_ref[...], preferred_element_type=F32) + ck_ref[...]).astype(BF16)


def _decay_scatter_constants():
    eq = np.zeros((3 * LANES, LANES), np.float32)
    ek = np.zeros((3 * LANES, LANES), np.float32)
    cq = np.zeros((1, LANES), np.float32)
    ck = np.zeros((1, LANES), np.float32)
    for h in range(N_HEADS):
        for p in range(3):
            eq[p * LANES + h, GATE_LANES * h + p] = 1.0
            ek[p * LANES + h, GATE_LANES * h + 3 + p] = -1.0
            cq[0, GATE_LANES * h + 3 + p] = 1.0
            ck[0, GATE_LANES * h + p] = 1.0
    return (jnp.asarray(eq, BF16), jnp.asarray(ek, BF16), jnp.asarray(cq), jnp.asarray(ck))


def _qkvf(h, g, w_q, w_k, w_vt, w_f, b_f):
    b, s, _ = h.shape
    rows = QKV_ROWS
    eq, ek, cq, ck = _decay_scatter_constants()
    tile = lambda width: pl.BlockSpec((1, rows, width), lambda i, j: (i, j, 0))
    return pl.pallas_call(
        _qkvf_kernel,
        grid=(b, s // rows),
        in_specs=[
            tile(D_MODEL),
            _const_spec((1, D_MODEL)),
            _const_spec((D_MODEL, D_MODEL)),
            _const_spec((D_MODEL, D_MODEL)),
            _const_spec((D_MODEL, D_MODEL)),
            _const_spec((D_MODEL, LANES)),
            _const_spec((1, LANES)),
            _const_spec((3 * LANES, LANES)),
            _const_spec((3 * LANES, LANES)),
            _const_spec((1, LANES)),
            _const_spec((1, LANES)),
        ],
        out_specs=[
            tile(D_MODEL), tile(LANES), tile(D_MODEL), tile(LANES),
            pl.BlockSpec((1, 1, D_MODEL, rows), lambda i, j: (i, j, 0, 0)),
        ],
        out_shape=[
            jax.ShapeDtypeStruct((b, s, D_MODEL), BF16),
            jax.ShapeDtypeStruct((b, s, LANES), BF16),
            jax.ShapeDtypeStruct((b, s, D_MODEL), BF16),
            jax.ShapeDtypeStruct((b, s, LANES), BF16),
            jax.ShapeDtypeStruct((b, s // rows, D_MODEL, rows), BF16),
        ],
        scratch_shapes=[pltpu.VMEM((SUBLANES, LANES), F32)],
        compiler_params=_params(2),
        name="qkvf",
    )(h, g, w_q, w_k, w_vt, w_f, b_f, eq, ek, cq, ck)


def _attn_kernel(q_ref, qc_ref, k_ref, kc_ref, vt_ref, o_ref):
    blk = q_ref.shape[1]
    qi = pl.program_id(2)
    pair = pl.program_id(1)
    q2 = q_ref[0]
    qc = qc_ref[0]
    lane = lax.broadcasted_iota(jnp.int32, (1, LANES), 1)
    zero = jnp.zeros((), BF16)
    q_aug = []
    for a in range(HEADS_PER_BLOCK):
        head = pair * HEADS_PER_BLOCK + a
        qm = jnp.where((lane >= a * HEAD_DIM) & (lane < (a + 1) * HEAD_DIM), q2, zero)
        qcm = jnp.where((lane >= head * GATE_LANES) & (lane < (head + 1) * GATE_LANES), qc, zero)
        q_aug.append(jnp.concatenate([qm, qcm], axis=1))
    ones_rows = jnp.ones((2 * SUBLANES, blk), BF16)

    def step(kb, carry, masked):
        k_aug = jnp.concatenate([k_ref[0, pl.ds(kb * blk, blk), :],
                                 kc_ref[0, pl.ds(kb * blk, blk), :]], axis=1)
        vt = vt_ref[0, kb]
        new = []
        for a in range(HEADS_PER_BLOCK):
            m, acc = carry[a]
            st = lax.dot_general(k_aug, q_aug[a], (((1,), (1,)), ((), ())),
                                 preferred_element_type=F32)
            if masked:
                kpos = lax.broadcasted_iota(jnp.int32, st.shape, 0)
                qpos = lax.broadcasted_iota(jnp.int32, st.shape, 1)
                st = jnp.where(kpos <= qpos, st, NEG_BIG)
            m_new = jnp.maximum(m, jnp.max(st, axis=0, keepdims=True))
            alpha = jnp.exp(m - m_new)
            pt = jnp.exp(st - m_new).astype(BF16)
            lhs = jnp.concatenate([vt[a * HEAD_DIM:(a + 1) * HEAD_DIM], ones_rows], axis=0)
            acc = alpha * acc + jnp.dot(lhs, pt, preferred_element_type=F32)
            new.append((m_new, acc))
        return tuple(new)

    init = tuple((jnp.full((1, blk), NEG_BIG, F32),
                  jnp.zeros((HEAD_DIM + 2 * SUBLANES, blk), F32))
                 for _ in range(HEADS_PER_BLOCK))
    carry = lax.fori_loop(0, qi, lambda kb, c: step(kb, c, False), init)
    carry = step(qi, carry, True)
    outs = [acc[:HEAD_DIM] / acc[HEAD_DIM:HEAD_DIM + 1] for _, acc in carry]
    o_ref[0] = jnp.concatenate(outs, axis=0).T.astype(BF16)


def _attn(q, qc, k, kc, vt):
    b, s, _ = q.shape
    blk = ATTN_BLOCK
    n_pairs = N_HEADS // HEADS_PER_BLOCK
    return pl.pallas_call(
        _attn_kernel,
        grid=(b, n_pairs, s // blk),
        in_specs=[
            pl.BlockSpec((1, blk, LANES), lambda i, p, j: (i, j, p)),
            pl.BlockSpec((1, blk, LANES), lambda i, p, j: (i, j, 0)),
            pl.BlockSpec((1, s, LANES), lambda i, p, j: (i, 0, p)),
            pl.BlockSpec((1, s, LANES), lambda i, p, j: (i, 0, 0)),
            pl.BlockSpec((1, s // blk, LANES, blk), lambda i, p, j: (i, 0, p, 0)),
        ],
        out_specs=pl.BlockSpec((1, blk, LANES), lambda i, p, j: (i, j, p)),
        out_shape=jax.ShapeDtypeStruct((b, s, D_MODEL), BF16),
        compiler_params=_params(3),
        name="fox_attn",
    )(q, qc, k, kc, vt)


def kernel(x, mix_norm_g, ffn_norm_g, gm_w_in, gm_ln_g, gm_ln_b, gm_w_s, gm_b_s, gm_w_out,
           fox_w_qkvf, fox_b_f, fox_w_o, ffn_w_gate, ffn_w_up, ffn_conv_w, ffn_conv_b,
           ffn_w_down, final_norm_g):
    assert QKV_ROWS == ATTN_BLOCK
    b, s, d = x.shape
    row = lambda v: v.reshape(1, -1).astype(F32)

    def ffn_args(i):
        conv_w = jnp.zeros((SUBLANES, FFN_DIM), F32).at[:CONV_WIDTH].set(ffn_conv_w[i])
        return (row(ffn_norm_g[i]), ffn_w_gate[i].astype(BF16), ffn_w_up[i].astype(BF16),
                conv_w, row(ffn_conv_b[i]), ffn_w_down[i].astype(BF16))

    b_s_b = jnp.broadcast_to(gm_b_s[0][:, :, None], (GM_GROUPS, CHUNK, GM_GROUP_DIM)).astype(F32)
    h = _gmlp(x.reshape(b * s, d), row(mix_norm_g[0]), gm_w_in[0].astype(BF16),
              row(gm_ln_g[0]), row(gm_ln_b[0]), gm_w_s[0].astype(BF16), b_s_b,
              gm_w_out[0].astype(BF16)).reshape(b, s, d)
    h = _ffn(h, *ffn_args(0))

    w = fox_w_qkvf[0]
    w_q = w[:, :d].astype(BF16)
    w_k = w[:, d:2 * d].astype(BF16)
    w_vt = w[:, 2 * d:3 * d].T.astype(BF16)
    w_f = jnp.zeros((d, LANES), F32).at[:, :N_HEADS].set(w[:, 3 * d:]).astype(BF16)
    b_f = jnp.zeros((1, LANES), F32).at[0, :N_HEADS].set(fox_b_f[0])
    q, qc, k, kc, vt = _qkvf(h, row(mix_norm_g[1]), w_q, w_k, w_vt, w_f, b_f)
    o = _attn(q, qc, k, kc, vt)
    return _ffn(h, *ffn_args(1), proj=(o, fox_w_o[0].astype(BF16)), final_g=row(final_norm_g))
```
